```python
import math
import jax, jax.numpy as jnp
from jax import lax
import numpy as np

D_MODEL = 2048
BATCH = 1
SEQ = 16384
DEPTH = 2

CHUNK = 64
Q_BLOCK = 128
HEAD_DIM = 128
N_HEADS = D_MODEL // HEAD_DIM
SB_HEADS = N_HEADS // 2
DIFF_HEADS = N_HEADS - SB_HEADS
SB_DIM = HEAD_DIM
DIFF_QK_DIM = HEAD_DIM // 2
DIFF_V_DIM = HEAD_DIM
ROT_DIM = DIFF_QK_DIM // 4
ROPE_THETA = 500000.0
SB_WIDTH = SB_HEADS * SB_DIM
DIFF_QK_WIDTH = DIFF_HEADS * 2 * DIFF_QK_DIM
DIFF_V_WIDTH = DIFF_HEADS * DIFF_V_DIM
MIX_WIDTH = SB_WIDTH + DIFF_V_WIDTH
IN_WIDTH = 3 * SB_WIDTH + 2 * DIFF_QK_WIDTH + DIFF_V_WIDTH
D_FF = (-(-8 * D_MODEL // 3) + 255) // 256 * 256
EPS = 1e-6

kernel_name = "stickbreak_diffattn_hybrid_encoder"


def lambda_init_value(layer):
    return 0.8 - 0.6 * math.exp(-0.3 * layer)


def rms_norm(x, g):
    xf = x.astype(jnp.float32)
    y = xf * lax.rsqrt(jnp.mean(xf * xf, axis=-1, keepdims=True) + EPS)
    return (y * g.astype(jnp.float32)).astype(x.dtype)


def partial_rope(x, pos):
    half = ROT_DIM // 2
    inv_freq = ROPE_THETA ** (-jnp.arange(0, ROT_DIM, 2, dtype=jnp.float32) / ROT_DIM)
    ang = pos.astype(jnp.float32)[:, None] * inv_freq[None, :]
    cos = jnp.cos(ang)[None, :, None, :]
    sin = jnp.sin(ang)[None, :, None, :]
    xr = x[..., :ROT_DIM].astype(jnp.float32)
    x1, x2 = xr[..., :half], xr[..., half:]
    rot = jnp.concatenate([x1 * cos - x2 * sin, x2 * cos + x1 * sin], axis=-1)
    return jnp.concatenate([rot.astype(x.dtype), x[..., ROT_DIM:]], axis=-1)


def to_blocks(t):
    b, s, h, d = t.shape
    return t.reshape(b, s // Q_BLOCK, Q_BLOCK, h, d).transpose(1, 0, 3, 2, 4)


def from_blocks(o):
    nb, b, h, qb, d = o.shape
    return o.transpose(1, 0, 3, 2, 4).reshape(b, nb * qb, h, d)


def stick_breaking_attention(q, k, v):
    s_len = q.shape[1]
    scale = 1.0 / math.sqrt(q.shape[-1])
    k_t = k.transpose(0, 2, 1, 3)
    v_t = v.transpose(0, 2, 1, 3)
    kpos = jnp.arange(s_len)

    def block(args):
        qb, i = args
        qpos = i * Q_BLOCK + jnp.arange(Q_BLOCK)
        z = jnp.einsum('bhqd,bhkd->bhqk', qb, k_t).astype(jnp.float32) * scale
        causal = kpos[None, :] < qpos[:, None]
        log_keep = jnp.where(causal, jax.nn.log_sigmoid(-z), 0.0)
        between = lax.cumsum(log_keep, axis=3, reverse=True) - log_keep
        w = jnp.where(causal, jnp.exp(jax.nn.log_sigmoid(z) + between), 0.0)
        return jnp.einsum('bhqk,bhkd->bhqd', w.astype(v.dtype), v_t)

    out = lax.map(block, (to_blocks(q), jnp.arange(s_len // Q_BLOCK)))
    return from_blocks(out)


def differential_attention(q1, q2, k1, k2, v, lam):
    s_len = q1.shape[1]
    scale = 1.0 / math.sqrt(q1.shape[-1])
    k1_t = k1.transpose(0, 2, 1, 3)
    k2_t = k2.transpose(0, 2, 1, 3)
    v_t = v.transpose(0, 2, 1, 3)
    kchunk = jnp.arange(s_len) // CHUNK

    def block(args):
        q1b, q2b, i = args
        qchunk = (i * Q_BLOCK + jnp.arange(Q_BLOCK)) // CHUNK
        allowed = kchunk[None, :] <= qchunk[:, None]

        def probs(qb, kt):
            sc = jnp.einsum('bhqd,bhkd->bhqk', qb, kt).astype(jnp.float32) * scale
            return jax.nn.softmax(jnp.where(allowed, sc, -jnp.inf), axis=-1)

        p = probs(q1b, k1_t) - lam * probs(q2b, k2_t)
        return jnp.einsum('bhqk,bhkd->bhqd', p.astype(v.dtype), v_t)

    out = lax.map(block, (to_blocks(q1), to_blocks(q2), jnp.arange(s_len // Q_BLOCK)))
    return from_blocks(out)


def hybrid_mixer(h, layer, w_in, w_o, sb_out_norm, diff_subln, lq1, lk1, lq2, lk2):
    b, s, _ = h.shape
    pos = jnp.arange(s)
    proj = h @ w_in
    offs = np.cumsum([SB_WIDTH, SB_WIDTH, SB_WIDTH, DIFF_QK_WIDTH, DIFF_QK_WIDTH])
    q_sb, k_sb, v_sb, q_d, k_d, v_d = jnp.split(proj, offs.tolist(), axis=-1)

    q_sb = q_sb.reshape(b, s, SB_HEADS, SB_DIM)
    k_sb = k_sb.reshape(b, s, SB_HEADS, SB_DIM)
    v_sb = v_sb.reshape(b, s, SB_HEADS, SB_DIM)
    o_sb = stick_breaking_attention(q_sb, k_sb, v_sb)
    o_sb = rms_norm(o_sb, sb_out_norm)

    q_d = partial_rope(q_d.reshape(b, s, DIFF_HEADS * 2, DIFF_QK_DIM), pos)
    k_d = partial_rope(k_d.reshape(b, s, DIFF_HEADS * 2, DIFF_QK_DIM), pos)
    q1, q2 = q_d[:, :, 0::2], q_d[:, :, 1::2]
    k1, k2 = k_d[:, :, 0::2], k_d[:, :, 1::2]
    v_d = v_d.reshape(b, s, DIFF_HEADS, DIFF_V_DIM)
    lam_init = lambda_init_value(layer)
    lam = (jnp.exp(jnp.sum(lq1.astype(jnp.float32) * lk1.astype(jnp.float32)))
           - jnp.exp(jnp.sum(lq2.astype(jnp.float32) * lk2.astype(jnp.float32)))
           + lam_init)
    o_d = differential_attention(q1, q2, k1, k2, v_d, lam)
    o_d = rms_norm(o_d, diff_subln) * (1.0 - lam_init)

    mix = jnp.concatenate([o_sb.reshape(b, s, SB_WIDTH),
                           o_d.reshape(b, s, DIFF_V_WIDTH)], axis=-1)
    return mix @ w_o


def swiglu(h, w_gate, w_up, w_down):
    return (jax.nn.silu(h @ w_gate) * (h @ w_up)) @ w_down


def setup_inputs(seed: int = 0) -> dict:
    key = jax.random.key(seed)
    ks = jax.random.split(key, 16)
    f32 = jnp.float32

    def nrm(k, shape, fan_in):
        return jax.random.normal(k, shape, f32) * (fan_in ** -0.5)

    def gain(k, shape):
        return 1.0 + 0.02 * jax.random.normal(k, shape, f32)

    return {
        "x": jax.random.normal(ks[0], (BATCH, SEQ, D_MODEL), f32),
        "w_in": nrm(ks[1], (DEPTH, D_MODEL, IN_WIDTH), D_MODEL),
        "w_o": nrm(ks[2], (DEPTH, MIX_WIDTH, D_MODEL), MIX_WIDTH),
        "sb_out_norm": gain(ks[3], (DEPTH, SB_HEADS, SB_DIM)),
        "diff_subln": gain(ks[4], (DEPTH, DIFF_HEADS, DIFF_V_DIM)),
        "lambda_q1": 0.1 * jax.random.normal(ks[5], (DEPTH, DIFF_QK_DIM), f32),
        "lambda_k1": 0.1 * jax.random.normal(ks[6], (DEPTH, DIFF_QK_DIM), f32),
        "lambda_q2": 0.1 * jax.random.normal(ks[7], (DEPTH, DIFF_QK_DIM), f32),
        "lambda_k2": 0.1 * jax.random.normal(ks[8], (DEPTH, DIFF_QK_DIM), f32),
        "pre_mix_norm": gain(ks[9], (DEPTH, D_MODEL)),
        "post_mix_norm": gain(ks[10], (DEPTH, D_MODEL)),
        "pre_ffn_norm": gain(ks[11], (DEPTH, D_MODEL)),
        "post_ffn_norm": gain(ks[12], (DEPTH, D_MODEL)),
        "w_gate": nrm(ks[13], (DEPTH, D_MODEL, D_FF), D_MODEL),
        "w_up": nrm(ks[14], (DEPTH, D_MODEL, D_FF), D_MODEL),
        "w_down": nrm(ks[15], (DEPTH, D_FF, D_MODEL), D_FF),
    }


def reference(x, w_in, w_o, sb_out_norm, diff_subln, lambda_q1, lambda_k1,
              lambda_q2, lambda_k2, pre_mix_norm, post_mix_norm, pre_ffn_norm,
              post_ffn_norm, w_gate, w_up, w_down):
    for l in range(DEPTH):
        h = rms_norm(x, pre_mix_norm[l])
        m = hybrid_mixer(h, l, w_in[l], w_o[l], sb_out_norm[l], diff_subln[l],
                         lambda_q1[l], lambda_k1[l], lambda_q2[l], lambda_k2[l])
        x = x + rms_norm(m, post_mix_norm[l])
        h = rms_norm(x, pre_ffn_norm[l])
        f = swiglu(h, w_gate[l], w_up[l], w_down[l])
        x = x + rms_norm(f, post_ffn_norm[l])
    return x
```

```python
import functools
import math

import jax
import jax.numpy as jnp
from jax import lax
from jax.experimental import pallas as pl
from jax.experimental.pallas import tpu as pltpu

F32 = jnp.float32
BF16 = jnp.bfloat16

HEAD_DIM = 128
DIFF_QK_DIM = 64
ROT_DIM = DIFF_QK_DIM // 4
ROPE_THETA = 500000.0
CHUNK = 64
EPS = 1e-6

LANES = 128
VMEM_LIMIT_BYTES = 60000 * 1024

SB_LOG_KEEP_FLOOR = -150.0
MASK_VALUE = -1e30


def _lambda_init(layer):
    return 0.8 - 0.6 * math.exp(-0.3 * layer)


def _params(n_axes):
    return pltpu.CompilerParams(
        dimension_semantics=("arbitrary",) * n_axes,
        vmem_limit_bytes=VMEM_LIMIT_BYTES,
    )


def _rms(x, g):
    ms = jnp.mean(x * x, axis=-1, keepdims=True)
    return x * lax.rsqrt(ms + EPS) * g


def _inproj_kernel(x_ref, g_ref, w_ref, cos_ref, sinm_ref, sinp_ref, proj_ref, vt_ref, h_ref,
                   *, sb_scale, diff_scale):
    n = pl.program_id(1)

    @pl.when(n == 0)
    def _():
        h_ref[...] = _rms(x_ref[...], g_ref[...]).astype(BF16)

    def project():
        return jnp.dot(h_ref[...], w_ref[...], preferred_element_type=F32)

    def rope(acc, scale):
        cos, sinm, sinp = cos_ref[...], sinm_ref[...], sinp_ref[...]
        for c in range(acc.shape[1] // LANES):
            xs = acc[:, c * LANES:(c + 1) * LANES]
            up = pltpu.roll(xs, LANES - ROT_DIM // 2, 1)
            down = pltpu.roll(xs, ROT_DIM // 2, 1)
            y = xs * cos + up * sinm + down * sinp
            proj_ref[:, c * LANES:(c + 1) * LANES] = (y * scale).astype(BF16)

    @pl.when(n == 0)
    def _():
        proj_ref[...] = (project() * sb_scale).astype(BF16)

    @pl.when((n == 1) | (n == 2))
    def _():
        proj_ref[...] = project().astype(BF16)

    @pl.when(n == 3)
    def _():
        rope(project(), diff_scale)

    @pl.when(n == 4)
    def _():
        rope(project(), 1.0)

    @pl.when(n == 5)
    def _():
        acc = project()
        proj_ref[...] = acc.astype(BF16)
        vt_ref[...] = acc.T.astype(BF16)


def _in_projection(x2d, g, w_bf16, cos_t, sinm_t, sinp_t):
    s, d = x2d.shape
    width = w_bf16.shape[1]
    tn = width // 6
    tm = min(1024, s)
    kern = functools.partial(_inproj_kernel, sb_scale=1.0 / math.sqrt(HEAD_DIM),
                             diff_scale=1.0 / math.sqrt(DIFF_QK_DIM))
    return pl.pallas_call(
        kern,
        grid=(s // tm, 6),
        in_specs=[
            pl.BlockSpec((tm, d), lambda i, n: (i, 0)),
            pl.BlockSpec((1, d), lambda i, n: (0, 0)),
            pl.BlockSpec((d, tn), lambda i, n: (0, n)),
            pl.BlockSpec((tm, LANES), lambda i, n: (i, 0)),
            pl.BlockSpec((tm, LANES), lambda i, n: (i, 0)),
            pl.BlockSpec((tm, LANES), lambda i, n: (i, 0)),
        ],
        out_specs=[
            pl.BlockSpec((tm, tn), lambda i, n: (i, n)),
            pl.BlockSpec((tn, tm), lambda i, n: (0, i)),
        ],
        out_shape=[
            jax.ShapeDtypeStruct((s, width), BF16),
            jax.ShapeDtypeStruct((tn, s), BF16),
        ],
        scratch_shapes=[pltpu.VMEM((tm, d), BF16)],
        compiler_params=_params(2),
        name="in_projection",
    )(x2d, g, w_bf16, cos_t, sinm_t, sinp_t)


def _sb_kernel(q_ref, k_ref, v_ref, g_ref, o_ref, *, tq):
    i = pl.program_id(1)
    q = q_ref[...]
    row = lax.broadcasted_iota(jnp.int32, (tq, tq), 0)
    col = lax.broadcasted_iota(jnp.int32, (tq, tq), 1)
    later = (row > col).astype(BF16)
    causal = col < row

    def sweep(j, carry, acc, diagonal):
        start = pl.multiple_of(j * tq, tq)
        k = k_ref[pl.ds(start, tq), :]
        v = v_ref[pl.ds(start, tq), :]
        z = lax.dot_general(q, k, (((1,), (1,)), ((), ())), preferred_element_type=F32)
        log_keep = -(jnp.maximum(z, 0.0) + jnp.log1p(jnp.exp(-jnp.abs(z))))
        if diagonal:
            log_keep = jnp.where(causal, log_keep, 0.0)
        hi = log_keep.astype(BF16)
        lo = (log_keep - hi.astype(F32)).astype(BF16)
        between = (jnp.dot(hi, later, preferred_element_type=F32)
                   + jnp.dot(lo, later, preferred_element_type=F32) + carry)
        w = jnp.exp(z + log_keep + between)
        if diagonal:
            w = jnp.where(causal, w, 0.0)
        acc = acc + jnp.dot(w.astype(BF16), v, preferred_element_type=F32)
        carry = carry + jnp.sum(log_keep, axis=1, keepdims=True)
        return carry, acc

    carry, acc = sweep(i, jnp.zeros((tq, 1), F32), jnp.zeros((tq, HEAD_DIM), F32), True)

    def cond(state):
        j, top, _, _ = state
        return (j >= 0) & (top > SB_LOG_KEEP_FLOOR)

    def body(state):
        j, _, carry, acc = state
        carry, acc = sweep(j, carry, acc, False)
        return j - 1, jnp.max(carry), carry, acc

    _, _, _, acc = lax.while_loop(cond, body, (i - 1, jnp.max(carry), carry, acc))
    o_ref[...] = _rms(acc, g_ref[0]).astype(BF16)


def _stick_breaking(proj, gain):
    s = proj.shape[0]
    heads = gain.shape[0]
    tq = min(256, s)
    return pl.pallas_call(
        functools.partial(_sb_kernel, tq=tq),
        grid=(heads, s // tq),
        in_specs=[
            pl.BlockSpec((tq, HEAD_DIM), lambda h, i: (i, h)),
            pl.BlockSpec((s, HEAD_DIM), lambda h, i: (0, heads + h)),
            pl.BlockSpec((s, HEAD_DIM), lambda h, i: (0, 2 * heads + h)),
            pl.BlockSpec((1, 1, HEAD_DIM), lambda h, i: (h, 0, 0)),
        ],
        out_specs=pl.BlockSpec((tq, HEAD_DIM), lambda h, i: (i, h)),
        out_shape=jax.ShapeDtypeStruct((s, heads * HEAD_DIM), BF16),
        compiler_params=_params(2),
        name="stick_breaking",
    )(proj, proj, proj, gain.reshape(heads, 1, HEAD_DIM))


def _diff_kernel(q_ref, k_ref, vt_ref, lq1_ref, lk1_ref, lq2_ref, lk2_ref, g_ref, o_ref,
                 acc1_ref, acc2_ref, *, tq, lam_init):
    i = pl.program_id(1)
    q = q_ref[...]
    lane = lax.broadcasted_iota(jnp.int32, (tq, LANES), 1)
    zero = jnp.zeros_like(q)
    q1 = jnp.where(lane < DIFF_QK_DIM, q, zero)
    q2 = jnp.where(lane >= DIFF_QK_DIM, q, zero)
    acc1_ref[...] = jnp.zeros_like(acc1_ref)
    acc2_ref[...] = jnp.zeros_like(acc2_ref)

    def update(s, m, l, acc_ref, vt):
        m_new = jnp.maximum(m, jnp.max(s, axis=0, keepdims=True))
        alpha = jnp.exp(m - m_new)
        p = jnp.exp(s - m_new)
        l = alpha * l + jnp.sum(p, axis=0, keepdims=True)
        acc_ref[...] = alpha * acc_ref[...] + jnp.dot(vt, p.astype(BF16), preferred_element_type=F32)
        return m_new, l

    def step(j, state, allowed):
        m1, l1, m2, l2 = state
        start = pl.multiple_of(j * tq, tq)
        k = k_ref[pl.ds(start, tq), :]
        vt = vt_ref[:, pl.ds(start, tq)]
        s1 = lax.dot_general(k, q1, (((1,), (1,)), ((), ())), preferred_element_type=F32)
        s2 = lax.dot_general(k, q2, (((1,), (1,)), ((), ())), preferred_element_type=F32)
        if allowed is not None:
            s1 = jnp.where(allowed, s1, MASK_VALUE)
            s2 = jnp.where(allowed, s2, MASK_VALUE)
        m1, l1 = update(s1, m1, l1, acc1_ref, vt)
        m2, l2 = update(s2, m2, l2, acc2_ref, vt)
        return m1, l1, m2, l2

    neg = jnp.full((1, tq), MASK_VALUE, F32)
    zer = jnp.zeros((1, tq), F32)
    state = lax.fori_loop(0, i, lambda j, st: step(j, st, None), (neg, zer, neg, zer))
    key_chunk = lax.broadcasted_iota(jnp.int32, (tq, tq), 0) // CHUNK
    qry_chunk = lax.broadcasted_iota(jnp.int32, (tq, tq), 1) // CHUNK
    _, l1, _, l2 = step(i, state, key_chunk <= qry_chunk)

    lam = (jnp.exp(jnp.sum(lq1_ref[...] * lk1_ref[...], axis=1, keepdims=True))
           - jnp.exp(jnp.sum(lq2_ref[...] * lk2_ref[...], axis=1, keepdims=True)) + lam_init)
    o = acc1_ref[...] / l1 - lam * (acc2_ref[...] / l2)
    ms = jnp.mean(o * o, axis=0, keepdims=True)
    y = (o * lax.rsqrt(ms + EPS)).T * g_ref[0] * (1.0 - lam_init)
    o_ref[...] = y.astype(BF16)


def _differential(proj, vt, lq1, lk1, lq2, lk2, gain, lam_init):
    s = proj.shape[0]
    heads = gain.shape[0]
    tq = min(256, s)
    q_col = 3 * heads
    k_col = 4 * heads
    vec = pl.BlockSpec((1, DIFF_QK_DIM), lambda h, i: (0, 0))
    return pl.pallas_call(
        functools.partial(_diff_kernel, tq=tq, lam_init=lam_init),
        grid=(heads, s // tq),
        in_specs=[
            pl.BlockSpec((tq, HEAD_DIM), lambda h, i: (i, q_col + h)),
            pl.BlockSpec((s, HEAD_DIM), lambda h, i: (0, k_col + h)),
            pl.BlockSpec((HEAD_DIM, s), lambda h, i: (h, 0)),
            vec, vec, vec, vec,
            pl.BlockSpec((1, 1, HEAD_DIM), lambda h, i: (h, 0, 0)),
        ],
        out_specs=pl.BlockSpec((tq, HEAD_DIM), lambda h, i: (i, h)),
        out_shape=jax.ShapeDtypeStruct((s, heads * HEAD_DIM), BF16),
        scratch_shapes=[pltpu.VMEM((HEAD_DIM, tq), F32), pltpu.VMEM((HEAD_DIM, tq), F32)],
        compiler_params=_params(2),
        name="differential",
    )(proj, proj, vt, lq1.reshape(1, -1), lk1.reshape(1, -1), lq2.reshape(1, -1), lk2.reshape(1, -1),
      gain.reshape(heads, 1, HEAD_DIM))


def _outproj_kernel(osb_ref, od_ref, wa_ref, wb_ref, x_ref, gpost_ref, gpre_ref, x1_ref, h2_ref):
    m = (jnp.dot(osb_ref[...], wa_ref[...], preferred_element_type=F32)
         + jnp.dot(od_ref[...], wb_ref[...], preferred_element_type=F32))
    x1 = x_ref[...] + _rms(m, gpost_ref[...])
    x1_ref[...] = x1
    h2_ref[...] = _rms(x1, gpre_ref[...]).astype(BF16)


def _out_projection(o_sb, o_d, w_o_bf16, x2d, g_post, g_pre):
    s, d = x2d.shape
    half = o_sb.shape[1]
    tm = min(512, s)
    row = lambda i: (i, 0)
    fixed = lambda i: (0, 0)
    return pl.pallas_call(
        _outproj_kernel,
        grid=(s // tm,),
        in_specs=[
            pl.BlockSpec((tm, half), row),
            pl.BlockSpec((tm, half), row),
            pl.BlockSpec((half, d), fixed),
            pl.BlockSpec((half, d), lambda i: (1, 0)),
            pl.BlockSpec((tm, d), row),
            pl.BlockSpec((1, d), fixed),
            pl.BlockSpec((1, d), fixed),
        ],
        out_specs=[pl.BlockSpec((tm, d), row), pl.BlockSpec((tm, d), row)],
        out_shape=[jax.ShapeDtypeStruct((s, d), F32), jax.ShapeDtypeStruct((s, d), BF16)],
        compiler_params=_params(1),
        name="out_projection",
    )(o_sb, o_d, w_o_bf16, w_o_bf16, x2d, g_post, g_pre)


def _ffn_kernel(h_ref, wg_ref, wu_ref, wd_ref, x_ref, g_ref, o_ref):
    f = pl.program_id(1)
    h = h_ref[...]
    gate = jnp.dot(h, wg_ref[...], preferred_element_type=F32)
    up = jnp.dot(h, wu_ref[...], preferred_element_type=F32)
    act = (gate / (1.0 + jnp.exp(-gate)) * up).astype(BF16)
    part = jnp.dot(act, wd_ref[...], preferred_element_type=F32)

    @pl.when(f == 0)
    def _():
        o_ref[...] = part

    @pl.when(f > 0)
    def _():
        o_ref[...] += part

    @pl.when(f == pl.num_programs(1) - 1)
    def _():
        o_ref[...] = x_ref[...] + _rms(o_ref[...], g_ref[...])


def _ffn(h2, wg, wu, wd, x1, g_post):
    s, d = x1.shape
    dff = wg.shape[1]
    tm = min(512, s)
    tf = 512
    return pl.pallas_call(
        _ffn_kernel,
        grid=(s // tm, dff // tf),
        in_specs=[
            pl.BlockSpec((tm, d), lambda i, f: (i, 0)),
            pl.BlockSpec((d, tf), lambda i, f: (0, f)),
            pl.BlockSpec((d, tf), lambda i, f: (0, f)),
            pl.BlockSpec((tf, d), lambda i, f: (f, 0)),
            pl.BlockSpec((tm, d), lambda i, f: (i, 0)),
            pl.BlockSpec((1, d), lambda i, f: (0, 0)),
        ],
        out_specs=pl.BlockSpec((tm, d), lambda i, f: (i, 0)),
        out_shape=jax.ShapeDtypeStruct((s, d), F32),
        compiler_params=_params(2),
        name="swiglu_ffn",
    )(h2, wg, wu, wd, x1, g_post)


def _rope_tables(s):
    half = ROT_DIM // 2
    inv_freq = ROPE_THETA ** (-jnp.arange(0, ROT_DIM, 2, dtype=F32) / ROT_DIM)
    ang = jnp.arange(s).astype(F32)[:, None] * inv_freq[None, :]
    cos, sin = jnp.cos(ang), jnp.sin(ang)
    one = jnp.ones((s, DIFF_QK_DIM - ROT_DIM), F32)
    nil = jnp.zeros((s, DIFF_QK_DIM - ROT_DIM), F32)
    z8 = jnp.zeros((s, half), F32)
    cos_h = jnp.concatenate([cos, cos, one], axis=1)
    sinm_h = jnp.concatenate([-sin, z8, nil], axis=1)
    sinp_h = jnp.concatenate([z8, sin, nil], axis=1)
    dup = lambda t: jnp.concatenate([t, t], axis=1)
    return dup(cos_h), dup(sinm_h), dup(sinp_h)


def kernel(x, w_in, w_o, sb_out_norm, diff_subln, lambda_q1, lambda_k1, lambda_q2, lambda_k2,
           pre_mix_norm, post_mix_norm, pre_ffn_norm, post_ffn_norm, w_gate, w_up, w_down):
    b, s, d = x.shape
    assert b == 1
    depth = w_in.shape[0]
    cos_t, sinm_t, sinp_t = _rope_tables(s)
    xc = x.reshape(s, d)
    for l in range(depth):
        row = lambda a: a[l].reshape(1, -1)
        proj, vt = _in_projection(xc, row(pre_mix_norm), w_in[l].astype(BF16), cos_t, sinm_t, sinp_t)
        o_sb = _stick_breaking(proj, sb_out_norm[l])
        o_d = _differential(proj, vt, lambda_q1[l], lambda_k1[l], lambda_q2[l], lambda_k2[l],
                            diff_subln[l], _lambda_init(l))
        x1, h2 = _out_projection(o_sb, o_d, w_o[l].astype(BF16), xc, row(post_mix_norm), row(pre_ffn_norm))
        xc = _ffn(h2, w_gate[l].astype(BF16), w_up[l].astype(BF16), w_down[l].astype(BF16), x1,
                  row(post_ffn_norm))
    return xc.reshape(b, s, d)
```
